```python
import jax, jax.numpy as jnp
from jax import lax
import numpy as np

D_MODEL = 2048
BATCH = 4
SEQ = 4096
DEPTH = 1

HEAD_DIM = 128
ATTN_WIDTH = D_MODEL // 2
ATTN_HEADS = ATTN_WIDTH // HEAD_DIM
CONV_WIDTH = D_MODEL - ATTN_WIDTH
CONV_GROUPS = CONV_WIDTH // HEAD_DIM
MIX_WIDTH = ATTN_WIDTH + CONV_WIDTH
IN_WIDTH = 3 * ATTN_WIDTH + 2 * CONV_WIDTH
DILATED_PATTERNS = ((128, 1), (512, 4), (2048, 16))
CONV_KERNEL = 31
D_FF = 5632
ROPE_THETA = 10000.0
NORM_EPS = 1e-6
N_MOD = 9
MASK_VALUE = -1e30

kernel_name = "hybrid_dilated_attn_conformer_conv_macaron"


def rms_norm(x, g):
    xf = x.astype(jnp.float32)
    y = xf * lax.rsqrt(jnp.mean(xf * xf, axis=-1, keepdims=True) + NORM_EPS)
    return (y * g.astype(jnp.float32)).astype(x.dtype)


def layer_norm(x, g, b):
    xf = x.astype(jnp.float32)
    mu = jnp.mean(xf, axis=-1, keepdims=True)
    xc = xf - mu
    y = xc * lax.rsqrt(jnp.mean(xc * xc, axis=-1, keepdims=True) + NORM_EPS)
    return (y * g.astype(jnp.float32) + b.astype(jnp.float32)).astype(x.dtype)


def modulate(h, shift, scale):
    return h * (1.0 + scale[:, None, :]) + shift[:, None, :]


def rope(t, pos):
    half = HEAD_DIM // 2
    inv_freq = ROPE_THETA ** (-jnp.arange(half, dtype=jnp.float32) / half)
    ang = pos.astype(jnp.float32)[:, None] * inv_freq[None, :]
    cos = jnp.cos(ang)[None, :, None, :]
    sin = jnp.sin(ang)[None, :, None, :]
    tf = t.astype(jnp.float32)
    t1, t2 = tf[..., :half], tf[..., half:]
    return jnp.concatenate([t1 * cos - t2 * sin, t2 * cos + t1 * sin], axis=-1).astype(t.dtype)


def dilated_window_attention(q, k, v, dilation, n_side):
    B, S, H, Dh = q.shape
    L = S // dilation
    blk = n_side
    nb = -(-L // blk)
    Lp = nb * blk

    def to_classes(t):
        return t.reshape(B, L, dilation, H, Dh).transpose(0, 2, 3, 1, 4)

    qc = jnp.pad(to_classes(q), ((0, 0), (0, 0), (0, 0), (0, Lp - L), (0, 0)))
    qb = qc.reshape(B, dilation, H, nb, blk, Dh)

    def key_blocks(t):
        tp = jnp.pad(to_classes(t), ((0, 0), (0, 0), (0, 0), (blk, Lp - L + blk), (0, 0)))
        tb = tp.reshape(B, dilation, H, nb + 2, blk, Dh)
        return jnp.concatenate([tb[:, :, :, :-2], tb[:, :, :, 1:-1], tb[:, :, :, 2:]], axis=4)

    kb = key_blocks(k)
    vb = key_blocks(v)

    m_q = jnp.arange(nb)[:, None] * blk + jnp.arange(blk)[None, :]
    m_k = jnp.arange(nb)[:, None] * blk - blk + jnp.arange(3 * blk)[None, :]
    rel = m_k[:, None, :] - m_q[:, :, None]
    valid = (jnp.abs(rel) <= n_side) & (m_k[:, None, :] >= 0) & (m_k[:, None, :] < L)

    s = jnp.einsum('bdhnqe,bdhnke->bdhnqk', qb, kb,
                   preferred_element_type=jnp.float32) * (Dh ** -0.5)
    s = jnp.where(valid, s, MASK_VALUE)
    lse = jax.nn.logsumexp(s, axis=-1)
    p = jnp.exp(s - lse[..., None])
    o = jnp.einsum('bdhnqk,bdhnke->bdhnqe', p, vb.astype(jnp.float32))

    o = o.reshape(B, dilation, H, Lp, Dh)[:, :, :, :L]
    o = o.transpose(0, 3, 1, 2, 4).reshape(B, S, H, Dh)
    lse = lse.reshape(B, dilation, H, Lp)[..., :L]
    lse = lse.transpose(0, 3, 1, 2).reshape(B, S, H)
    return o, lse


def swiglu(h, w_gate, w_up, w_down):
    return (jax.nn.silu(h @ w_gate) * (h @ w_up)) @ w_down


def setup_inputs(seed: int = 0) -> dict:
    key = jax.random.key(seed)
    ks = jax.random.split(key, 32)
    f32 = jnp.float32

    def nrm(k, shape, fan_in):
        return jax.random.normal(k, shape, f32) * (fan_in ** -0.5)

    def gain(k, n):
        return 1.0 + 0.01 * jax.random.normal(k, (DEPTH, n), f32)

    def small(k, n):
        return 0.01 * jax.random.normal(k, (DEPTH, n), f32)

    return {
        "x": jax.random.normal(ks[0], (BATCH, SEQ, D_MODEL), f32),
        "c": jax.random.normal(ks[1], (BATCH, D_MODEL), f32),
        "w_ada": nrm(ks[2], (DEPTH, D_MODEL, N_MOD * D_MODEL), D_MODEL),
        "b_ada": small(ks[3], N_MOD * D_MODEL),
        "ffn1_pre_g": gain(ks[4], D_MODEL),
        "ffn1_w_gate": nrm(ks[5], (DEPTH, D_MODEL, D_FF), D_MODEL),
        "ffn1_w_up": nrm(ks[6], (DEPTH, D_MODEL, D_FF), D_MODEL),
        "ffn1_w_down": nrm(ks[7], (DEPTH, D_FF, D_MODEL), D_FF),
        "ffn1_post_g": gain(ks[8], D_MODEL),
        "mix_pre_g": gain(ks[9], D_MODEL),
        "w_in": nrm(ks[10], (DEPTH, D_MODEL, IN_WIDTH), D_MODEL),
        "conv_w": nrm(ks[11], (DEPTH, CONV_KERNEL, CONV_WIDTH), CONV_KERNEL),
        "conv_b": small(ks[12], CONV_WIDTH),
        "conv_ln_g": gain(ks[13], CONV_WIDTH),
        "conv_ln_b": small(ks[14], CONV_WIDTH),
        "attn_out_g": gain(ks[15], ATTN_WIDTH),
        "conv_out_g": gain(ks[16], CONV_WIDTH),
        "w_out": nrm(ks[17], (DEPTH, MIX_WIDTH, D_MODEL), MIX_WIDTH),
        "mix_post_g": gain(ks[18], D_MODEL),
        "ffn2_pre_g": gain(ks[19], D_MODEL),
        "ffn2_w_gate": nrm(ks[20], (DEPTH, D_MODEL, D_FF), D_MODEL),
        "ffn2_w_up": nrm(ks[21], (DEPTH, D_MODEL, D_FF), D_MODEL),
        "ffn2_w_down": nrm(ks[22], (DEPTH, D_FF, D_MODEL), D_FF),
        "ffn2_post_g": gain(ks[23], D_MODEL),
    }


def reference(x, c, w_ada, b_ada, ffn1_pre_g, ffn1_w_gate, ffn1_w_up, ffn1_w_down,
              ffn1_post_g, mix_pre_g, w_in, conv_w, conv_b, conv_ln_g, conv_ln_b,
              attn_out_g, conv_out_g, w_out, mix_post_g, ffn2_pre_g, ffn2_w_gate,
              ffn2_w_up, ffn2_w_down, ffn2_post_g):
    B, S, D = x.shape
    pos = jnp.arange(S, dtype=jnp.int32)
    c_act = jax.nn.silu(c)

    for l in range(DEPTH):
        mod = c_act @ w_ada[l] + b_ada[l]
        (sh1, sc1, g1, sh2, sc2, g2, sh3, sc3, g3) = jnp.split(mod, N_MOD, axis=-1)

        h = modulate(rms_norm(x, ffn1_pre_g[l]), sh1, sc1)
        f = swiglu(h, ffn1_w_gate[l], ffn1_w_up[l], ffn1_w_down[l])
        x = x + 0.5 * g1[:, None, :] * rms_norm(f, ffn1_post_g[l])

        h = modulate(rms_norm(x, mix_pre_g[l]), sh2, sc2)
        proj = h @ w_in[l]
        q, k, v, c_val, c_gate = jnp.split(
            proj, [ATTN_WIDTH, 2 * ATTN_WIDTH, 3 * ATTN_WIDTH, 3 * ATTN_WIDTH + CONV_WIDTH],
            axis=-1)
        q = rope(q.reshape(B, S, ATTN_HEADS, HEAD_DIM), pos)
        k = rope(k.reshape(B, S, ATTN_HEADS, HEAD_DIM), pos)
        v = v.reshape(B, S, ATTN_HEADS, HEAD_DIM)

        outs, lses = [], []
        for window, dilation in DILATED_PATTERNS:
            o_i, lse_i = dilated_window_attention(q, k, v, dilation, window // (2 * dilation))
            outs.append(o_i)
            lses.append(lse_i)
        wts = jax.nn.softmax(jnp.stack(lses, axis=0), axis=0)
        attn = jnp.einsum('pbsh,pbshe->bshe', wts, jnp.stack(outs, axis=0))
        attn = attn.reshape(B, S, ATTN_WIDTH).astype(x.dtype)

        u = c_val * jax.nn.sigmoid(c_gate)
        u = lax.conv_general_dilated(
            u, conv_w[l][:, None, :].astype(u.dtype), window_strides=(1,),
            padding=[((CONV_KERNEL - 1) // 2, (CONV_KERNEL - 1) // 2)],
            dimension_numbers=('NWC', 'WIO', 'NWC'),
            feature_group_count=CONV_WIDTH) + conv_b[l]
        u = jax.nn.silu(layer_norm(u, conv_ln_g[l], conv_ln_b[l]))

        merged = jnp.concatenate([rms_norm(attn, attn_out_g[l]),
                                  rms_norm(u, conv_out_g[l])], axis=-1) @ w_out[l]
        x = x + g2[:, None, :] * rms_norm(merged, mix_post_g[l])

        h = modulate(rms_norm(x, ffn2_pre_g[l]), sh3, sc3)
        f = swiglu(h, ffn2_w_gate[l], ffn2_w_up[l], ffn2_w_down[l])
        x = x + 0.5 * g3[:, None, :] * rms_norm(f, ffn2_post_g[l])

    return x
```

```python
import functools

import jax
import jax.numpy as jnp
import numpy as np
from jax import lax
from jax.experimental import pallas as pl
from jax.experimental.pallas import tpu as pltpu

HEAD_DIM = 128
DILATED_PATTERNS = ((128, 1), (512, 4), (2048, 16))
CONV_KERNEL = 31
ROPE_THETA = 10000.0
NORM_EPS = 1e-6
N_MOD = 9
MASK_VALUE = -1e30

F32 = jnp.float32
BF16 = jnp.bfloat16

Q_BLOCK = 128
N_SIDE = 64
K_BLOCK = Q_BLOCK + 2 * N_SIDE
BLOCKS_PER_ITER = 4
CONV_HALO = 16


def _params(dims, vmem_mb):
    return pltpu.CompilerParams(dimension_semantics=dims,
                                vmem_limit_bytes=vmem_mb * 1024 * 1024)


def _rms(x, g):
    return x * lax.rsqrt(jnp.mean(x * x, axis=-1, keepdims=True) + NORM_EPS) * g


def _sigmoid(x):
    return 1.0 / (1.0 + jnp.exp(-x))


def _adaln_kernel(c_ref, w_ref, b_ref, o_ref):
    c = c_ref[...]
    ca = (c * _sigmoid(c)).astype(BF16)
    o_ref[...] = jnp.dot(ca, w_ref[...].astype(BF16), preferred_element_type=F32) + b_ref[...]


def _adaln(c8, w, b, tn=1024):
    rows, d = c8.shape
    n = w.shape[1]
    return pl.pallas_call(
        _adaln_kernel,
        grid=(n // tn,),
        in_specs=[pl.BlockSpec((rows, d), lambda j: (0, 0)),
                  pl.BlockSpec((d, tn), lambda j: (0, j)),
                  pl.BlockSpec((1, tn), lambda j: (0, j))],
        out_specs=pl.BlockSpec((rows, tn), lambda j: (0, j)),
        out_shape=jax.ShapeDtypeStruct((rows, n), F32),
        compiler_params=_params(("arbitrary",), 40),
        name="adaln",
    )(c8, w, b)


def _ffn_kernel(x_ref, pre_g_ref, sh_ref, sc_ref, gate_ref, wg_ref, wu_ref, wd_ref,
                post_g_ref, o_ref, h_ref, acc_ref):
    j = pl.program_id(1)

    @pl.when(j == 0)
    def _():
        h = _rms(x_ref[...], pre_g_ref[...]) * (1.0 + sc_ref[...]) + sh_ref[...]
        h_ref[...] = h.astype(BF16)
        acc_ref[...] = jnp.zeros_like(acc_ref)

    h = h_ref[...]
    g = jnp.dot(h, wg_ref[...], preferred_element_type=F32)
    u = jnp.dot(h, wu_ref[...], preferred_element_type=F32)
    a = (g * _sigmoid(g) * u).astype(BF16)
    acc_ref[...] += jnp.dot(a, wd_ref[...], preferred_element_type=F32)

    @pl.when(j == pl.num_programs(1) - 1)
    def _():
        o_ref[...] = x_ref[...] + 0.5 * gate_ref[...] * _rms(acc_ref[...], post_g_ref[...])


def _ffn(x, pre_g, sh, sc, gate, wg, wu, wd, post_g, seq, tm=512, tf=512):
    n, d = x.shape
    dff = wg.shape[1]
    tpb = seq // tm
    row = lambda i, j: (i, 0)
    vec = lambda i, j: (0, 0)
    per_b = lambda i, j: (i // tpb, 0, 0)
    return pl.pallas_call(
        _ffn_kernel,
        grid=(n // tm, dff // tf),
        in_specs=[pl.BlockSpec((tm, d), row),
                  pl.BlockSpec((1, d), vec),
                  pl.BlockSpec((None, 1, d), per_b),
                  pl.BlockSpec((None, 1, d), per_b),
                  pl.BlockSpec((None, 1, d), per_b),
                  pl.BlockSpec((d, tf), lambda i, j: (0, j)),
                  pl.BlockSpec((d, tf), lambda i, j: (0, j)),
                  pl.BlockSpec((tf, d), lambda i, j: (j, 0)),
                  pl.BlockSpec((1, d), vec)],
        out_specs=pl.BlockSpec((tm, d), row),
        out_shape=jax.ShapeDtypeStruct((n, d), F32),
        scratch_shapes=[pltpu.VMEM((tm, d), BF16), pltpu.VMEM((tm, d), F32)],
        compiler_params=_params(("parallel", "arbitrary"), 56),
        name="ffn",
    )(x, pre_g, sh, sc, gate, wg, wu, wd, post_g)


def _in_proj_kernel(x_ref, pre_g_ref, sh_ref, sc_ref, w_ref, cos_ref, sin_ref,
                    q_ref, k_ref, v_ref, u_ref, h_ref, cval_ref):
    j = pl.program_id(1)

    @pl.when(j == 0)
    def _():
        h = _rms(x_ref[...], pre_g_ref[...]) * (1.0 + sc_ref[...]) + sh_ref[...]
        h_ref[...] = h.astype(BF16)

    p = jnp.dot(h_ref[...], w_ref[...], preferred_element_type=F32)
    n_heads = p.shape[1] // HEAD_DIM

    def rope(out_ref, scale):
        cos = cos_ref[...]
        sin = sin_ref[...]
        for hd in range(n_heads):
            sl = slice(hd * HEAD_DIM, (hd + 1) * HEAD_DIM)
            t = p[:, sl]
            r = t * cos + pltpu.roll(t, HEAD_DIM // 2, axis=1) * sin
            if scale != 1.0:
                r = r * scale
            out_ref[:, sl] = r.astype(out_ref.dtype)

    @pl.when(j == 0)
    def _():
        rope(q_ref, HEAD_DIM ** -0.5)

    @pl.when(j == 1)
    def _():
        rope(k_ref, 1.0)

    @pl.when(j == 2)
    def _():
        v_ref[...] = p.astype(v_ref.dtype)

    @pl.when(j == 3)
    def _():
        cval_ref[...] = p

    @pl.when(j == 4)
    def _():
        u_ref[...] = cval_ref[...] * _sigmoid(p)


def _in_proj(x, pre_g, sh, sc, w5, cosf, sinf, seq, tm=512):
    n, d = x.shape
    wdt = w5.shape[2]
    tpb = seq // tm
    row = lambda i, j: (i, 0)
    vec = lambda i, j: (0, 0)
    per_b = lambda i, j: (i // tpb, 0, 0)
    pos = lambda i, j: (i % tpb, 0)
    return pl.pallas_call(
        _in_proj_kernel,
        grid=(n // tm, w5.shape[0]),
        in_specs=[pl.BlockSpec((tm, d), row),
                  pl.BlockSpec((1, d), vec),
                  pl.BlockSpec((None, 1, d), per_b),
                  pl.BlockSpec((None, 1, d), per_b),
                  pl.BlockSpec((None, d, wdt), lambda i, j: (j, 0, 0)),
                  pl.BlockSpec((tm, HEAD_DIM), pos),
                  pl.BlockSpec((tm, HEAD_DIM), pos)],
        out_specs=[pl.BlockSpec((tm, wdt), row)] * 4,
        out_shape=[jax.ShapeDtypeStruct((n, wdt), BF16)] * 3
                  + [jax.ShapeDtypeStruct((n, wdt), F32)],
        scratch_shapes=[pltpu.VMEM((tm, d), BF16), pltpu.VMEM((tm, wdt), F32)],
        compiler_params=_params(("parallel", "arbitrary"), 48),
        name="in_proj",
    )(x, pre_g, sh, sc, w5, cosf, sinf)


def _attn_kernel(q1_ref, k1_ref, v1_ref, q4_ref, k4_ref, v4_ref, q16_ref, k16_ref, v16_ref,
                 o_ref, acc_ref, m_ref, bias_ref):
    seq = o_ref.shape[0]
    rel = (lax.broadcasted_iota(jnp.int32, (Q_BLOCK, K_BLOCK), 1)
           - lax.broadcasted_iota(jnp.int32, (Q_BLOCK, K_BLOCK), 0))
    for variant, off in enumerate((0, -N_SIDE, -2 * N_SIDE)):
        ok = jnp.abs(rel + off) <= N_SIDE
        bias_ref[variant] = jnp.where(ok, 0.0, MASK_VALUE).astype(F32)

    ones = jnp.ones((K_BLOCK, HEAD_DIM), BF16)

    def run_pattern(refs, dil, first):
        q_ref, k_ref, v_ref = refs
        length = seq // dil
        nblk = length // Q_BLOCK
        shift = nblk.bit_length() - 1

        def one_block(bidx):
            r = bidx >> shift
            jb = bidx & (nblk - 1)
            m0 = pl.multiple_of(jb * Q_BLOCK, Q_BLOCK)
            ks = pl.multiple_of(jnp.clip(m0 - N_SIDE, 0, length - K_BLOCK), N_SIDE)
            variant = jnp.where(jb == 0, 0, jnp.where(jb == nblk - 1, 2, 1))
            if dil == 1:
                qb = q_ref[pl.ds(m0, Q_BLOCK), :]
                kb = k_ref[pl.ds(ks, K_BLOCK), :]
                vb = v_ref[pl.ds(ks, K_BLOCK), :]
            else:
                qb = q_ref[r, pl.ds(m0, Q_BLOCK), :]
                kb = k_ref[r, pl.ds(ks, K_BLOCK), :]
                vb = v_ref[r, pl.ds(ks, K_BLOCK), :]
            s = lax.dot_general(qb, kb, (((1,), (1,)), ((), ())),
                                preferred_element_type=F32) + bias_ref[variant]
            row_max = jnp.max(s, axis=1, keepdims=True)
            v_aug = jnp.concatenate([vb, ones], axis=1)
            if first:
                rows = pl.ds(m0, Q_BLOCK)
                m_new = jnp.broadcast_to(row_max, (Q_BLOCK, HEAD_DIM))
            else:
                rows = pl.ds(m0 * dil + r, Q_BLOCK, stride=dil)
                m_old = m_ref[rows, :]
                m_new = jnp.maximum(m_old, row_max)
                alpha = jnp.exp(m_old - m_new)
            p = jnp.exp(s - jnp.concatenate([m_new, m_new], axis=1)).astype(BF16)
            pv = jnp.dot(p, v_aug, preferred_element_type=F32)
            if first:
                acc_ref[0, rows, :] = pv[:, :HEAD_DIM]
                acc_ref[1, rows, :] = pv[:, HEAD_DIM:]
            else:
                acc_ref[0, rows, :] = acc_ref[0, rows, :] * alpha + pv[:, :HEAD_DIM]
                acc_ref[1, rows, :] = acc_ref[1, rows, :] * alpha + pv[:, HEAD_DIM:]
            m_ref[rows, :] = m_new

        def body(it, carry):
            for g in range(BLOCKS_PER_ITER):
                one_block(it * BLOCKS_PER_ITER + g)
            return carry

        lax.fori_loop(0, (seq // Q_BLOCK) // BLOCKS_PER_ITER, body, 0)

    run_pattern((q1_ref, k1_ref, v1_ref), 1, True)
    run_pattern((q4_ref, k4_ref, v4_ref), 4, False)
    run_pattern((q16_ref, k16_ref, v16_ref), 16, False)
    o_ref[...] = (acc_ref[0] / acc_ref[1]).astype(o_ref.dtype)


def _attention(q, k, v, classes, batch, seq, heads):
    nat = pl.BlockSpec((None, seq, HEAD_DIM), lambda b, h: (b, 0, h))
    in_specs = [nat, nat, nat]
    args = [q, k, v]
    for dil, arrs in classes:
        spec = pl.BlockSpec((None, None, dil, seq // dil, HEAD_DIM), lambda b, h: (b, h, 0, 0, 0))
        in_specs += [spec] * 3
        args += list(arrs)
    return pl.pallas_call(
        _attn_kernel,
        grid=(batch, heads),
        in_specs=in_specs,
        out_specs=nat,
        out_shape=jax.ShapeDtypeStruct((batch, seq, heads * HEAD_DIM), F32),
        scratch_shapes=[pltpu.VMEM((2, seq, HEAD_DIM), F32),
                        pltpu.VMEM((seq, HEAD_DIM), F32),
                        pltpu.VMEM((3, Q_BLOCK, K_BLOCK), F32)],
        compiler_params=_params(("parallel", "parallel"), 48),
        name="attention",
    )(*args)


def _mix_out_kernel(attn_ref, u_ref, uprev_ref, unext_ref, x_ref, cw_ref, cb_ref, lng_ref,
                    lnb_ref, ag_ref, cg_ref, wo_ref, post_g_ref, gate_ref, o_ref, ubuf_ref,
                    *, tiles_per_seq):
    tm, cw = u_ref.shape
    t = pl.program_id(0) % tiles_per_seq
    keep_prev = (t != 0).astype(F32)
    keep_next = (t != tiles_per_seq - 1).astype(F32)
    ubuf_ref[0:CONV_HALO, :] = uprev_ref[...] * keep_prev
    ubuf_ref[CONV_HALO:CONV_HALO + tm, :] = u_ref[...]
    ubuf_ref[CONV_HALO + tm:, :] = unext_ref[...] * keep_next

    pad = (CONV_KERNEL - 1) // 2
    conv = jnp.broadcast_to(cb_ref[...], (tm, cw))
    for tap in range(CONV_KERNEL):
        conv = conv + cw_ref[tap:tap + 1, :] * ubuf_ref[pl.ds(CONV_HALO - pad + tap, tm), :]

    mu = jnp.mean(conv, axis=-1, keepdims=True)
    xc = conv - mu
    y = xc * lax.rsqrt(jnp.mean(xc * xc, axis=-1, keepdims=True) + NORM_EPS)
    y = y * lng_ref[...] + lnb_ref[...]
    y = y * _sigmoid(y)
    a_conv = _rms(y, cg_ref[...]).astype(BF16)
    a_attn = _rms(attn_ref[...], ag_ref[...]).astype(BF16)
    aw = attn_ref.shape[1]
    merged = (jnp.dot(a_attn, wo_ref[0:aw, :], preferred_element_type=F32)
              + jnp.dot(a_conv, wo_ref[aw:, :], preferred_element_type=F32))
    o_ref[...] = x_ref[...] + gate_ref[...] * _rms(merged, post_g_ref[...])


def _mix_out(attn, u, x, conv_w, conv_b, ln_g, ln_b, attn_g, conv_g, w_out, post_g, gate,
             seq, tm=512):
    n, d = x.shape
    aw = attn.shape[1]
    cw = u.shape[1]
    tpb = seq // tm
    hb = tm // CONV_HALO
    last_halo = n // CONV_HALO - 1
    row = lambda i: (i, 0)
    vec = lambda i: (0, 0)
    return pl.pallas_call(
        functools.partial(_mix_out_kernel, tiles_per_seq=tpb),
        grid=(n // tm,),
        in_specs=[pl.BlockSpec((tm, aw), row),
                  pl.BlockSpec((tm, cw), row),
                  pl.BlockSpec((CONV_HALO, cw), lambda i: (jnp.maximum(i * hb - 1, 0), 0)),
                  pl.BlockSpec((CONV_HALO, cw), lambda i: (jnp.minimum((i + 1) * hb, last_halo), 0)),
                  pl.BlockSpec((tm, d), row),
                  pl.BlockSpec((CONV_KERNEL, cw), vec),
                  pl.BlockSpec((1, cw), vec),
                  pl.BlockSpec((1, cw), vec),
                  pl.BlockSpec((1, cw), vec),
                  pl.BlockSpec((1, aw), vec),
                  pl.BlockSpec((1, cw), vec),
                  pl.BlockSpec((aw + cw, d), vec),
                  pl.BlockSpec((1, d), vec),
                  pl.BlockSpec((None, 1, d), lambda i: (i // tpb, 0, 0))],
        out_specs=pl.BlockSpec((tm, d), row),
        out_shape=jax.ShapeDtypeStruct((n, d), F32),
        scratch_shapes=[pltpu.VMEM((tm + 2 * CONV_HALO, cw), F32)],
        compiler_params=_params(("parallel",), 56),
        name="mix_out",
    )(attn, u, u, u, x, conv_w, conv_b, ln_g, ln_b, attn_g, conv_g, w_out, post_g, gate)


def _rope_tables(seq):
    half = HEAD_DIM // 2
    inv_freq = ROPE_THETA ** (-jnp.arange(half, dtype=F32) / half)
    ang = jnp.arange(seq, dtype=jnp.int32).astype(F32)[:, None] * inv_freq[None, :]
    cos, sin = jnp.cos(ang), jnp.sin(ang)
    return jnp.concatenate([cos, cos], axis=-1), jnp.concatenate([-sin, sin], axis=-1)


def kernel(x, c, w_ada, b_ada, ffn1_pre_g, ffn1_w_gate, ffn1_w_up, ffn1_w_down, ffn1_post_g, mix_pre_g, w_in, conv_w, conv_b, conv_ln_g, conv_ln_b, attn_out_g, conv_out_g, w_out, mix_post_g, ffn2_pre_g, ffn2_w_gate, ffn2_w_up, ffn2_w_down, ffn2_post_g):
    batch, seq, d = x.shape
    depth = w_ada.shape[0]
    attn_w = attn_out_g.shape[1]
    heads = attn_w // HEAD_DIM
    n = batch * seq
    cosf, sinf = _rope_tables(seq)
    c8 = jnp.pad(c, ((0, 8 - batch), (0, 0)))
    row = lambda a: a.reshape(1, -1)
    xf = x.reshape(n, d)

    for l in range(depth):
        mod = _adaln(c8, w_ada[l], row(b_ada[l]))[:batch]
        sh1, sc1, g1, sh2, sc2, g2, sh3, sc3, g3 = [
            m.reshape(batch, 1, d) for m in jnp.split(mod, N_MOD, axis=-1)]

        xf = _ffn(xf, row(ffn1_pre_g[l]), sh1, sc1, g1, ffn1_w_gate[l].astype(BF16),
                  ffn1_w_up[l].astype(BF16), ffn1_w_down[l].astype(BF16),
                  row(ffn1_post_g[l]), seq)

        w5 = w_in[l].reshape(d, -1, attn_w).transpose(1, 0, 2).astype(BF16)
        q, k, v, u = _in_proj(xf, row(mix_pre_g[l]), sh2, sc2, w5, cosf, sinf, seq)

        def to_classes(t, dil):
            return t.reshape(batch, seq // dil, dil, heads, HEAD_DIM).transpose(0, 3, 2, 1, 4)

        classes = [(dil, [to_classes(t, dil) for t in (q, k, v)])
                   for _, dil in DILATED_PATTERNS[1:]]
        shp = (batch, seq, attn_w)
        attn = _attention(q.reshape(shp), k.reshape(shp), v.reshape(shp), classes,
                          batch, seq, heads)

        xf = _mix_out(attn.reshape(n, attn_w), u, xf, conv_w[l], row(conv_b[l]),
                      row(conv_ln_g[l]), row(conv_ln_b[l]), row(attn_out_g[l]),
                      row(conv_out_g[l]), w_out[l].astype(BF16), row(mix_post_g[l]), g2, seq)

        xf = _ffn(xf, row(ffn2_pre_g[l]), sh3, sc3, g3, ffn2_w_gate[l].astype(BF16),
                  ffn2_w_up[l].astype(BF16), ffn2_w_down[l].astype(BF16),
                  row(ffn2_post_g[l]), seq)

    return xf.reshape(batch, seq, d)
```

```python
import functools

import jax
import jax.numpy as jnp
import numpy as np
from jax import lax
from jax.experimental import pallas as pl
from jax.experimental.pallas import tpu as pltpu

HEAD_DIM = 128
DILATED_PATTERNS = ((128, 1), (512, 4), (2048, 16))
CONV_KERNEL = 31
ROPE_THETA = 10000.0
NORM_EPS = 1e-6
N_MOD = 9
MASK_VALUE = -1e30

F32 = jnp.float32
BF16 = jnp.bfloat16

Q_BLOCK = 128
N_SIDE = 64
K_BLOCK = Q_BLOCK + 2 * N_SIDE
BLOCKS_PER_ITER = 16
CONV_HALO = 16
CONV_ROWS = 128


def _params(dims, vmem_mb):
    return pltpu.CompilerParams(dimension_semantics=dims,
                                vmem_limit_bytes=vmem_mb * 1024 * 1024)


def _rms(x, g):
    return x * lax.rsqrt(jnp.mean(x * x, axis=-1, keepdims=True) + NORM_EPS) * g


def _sigmoid(x):
    return 1.0 / (1.0 + jnp.exp(-x))


def _adaln_kernel(c_ref, w_ref, b_ref, o_ref):
    c = c_ref[...]
    ca = (c * _sigmoid(c)).astype(BF16)
    o_ref[...] = jnp.dot(ca, w_ref[...].astype(BF16), preferred_element_type=F32) + b_ref[...]


def _adaln(c8, w, b, tn=1024):
    rows, d = c8.shape
    n = w.shape[1]
    return pl.pallas_call(
        _adaln_kernel,
        grid=(n // tn,),
        in_specs=[pl.BlockSpec((rows, d), lambda j: (0, 0)),
                  pl.BlockSpec((d, tn), lambda j: (0, j)),
                  pl.BlockSpec((1, tn), lambda j: (0, j))],
        out_specs=pl.BlockSpec((rows, tn), lambda j: (0, j)),
        out_shape=jax.ShapeDtypeStruct((rows, n), F32),
        compiler_params=_params(("arbitrary",), 40),
        name="adaln",
    )(c8, w, b)


def _ffn_kernel(x_ref, pre_g_ref, sh_ref, sc_ref, gate_ref, wg_ref, wu_ref, wd_ref,
                post_g_ref, o_ref, h_ref, acc_ref):
    j = pl.program_id(1)

    @pl.when(j == 0)
    def _():
        h = _rms(x_ref[...], pre_g_ref[...]) * (1.0 + sc_ref[...]) + sh_ref[...]
        h_ref[...] = h.astype(BF16)
        acc_ref[...] = jnp.zeros_like(acc_ref)

    h = h_ref[...]
    g = jnp.dot(h, wg_ref[...], preferred_element_type=F32)
    u = jnp.dot(h, wu_ref[...], preferred_element_type=F32)
    a = (g * _sigmoid(g) * u).astype(BF16)
    acc_ref[...] += jnp.dot(a, wd_ref[...], preferred_element_type=F32)

    @pl.when(j == pl.num_programs(1) - 1)
    def _():
        o_ref[...] = x_ref[...] + 0.5 * gate_ref[...] * _rms(acc_ref[...], post_g_ref[...])


def _ffn(x, pre_g, sh, sc, gate, wg, wu, wd, post_g, seq, tm=512, tf=512):
    n, d = x.shape
    dff = wg.shape[1]
    tpb = seq // tm
    row = lambda i, j: (i, 0)
    vec = lambda i, j: (0, 0)
    per_b = lambda i, j: (i // tpb, 0, 0)
    return pl.pallas_call(
        _ffn_kernel,
        grid=(n // tm, dff // tf),
        in_specs=[pl.BlockSpec((tm, d), row),
                  pl.BlockSpec((1, d), vec),
                  pl.BlockSpec((None, 1, d), per_b),
                  pl.BlockSpec((None, 1, d), per_b),
                  pl.BlockSpec((None, 1, d), per_b),
                  pl.BlockSpec((d, tf), lambda i, j: (0, j)),
                  pl.BlockSpec((d, tf), lambda i, j: (0, j)),
                  pl.BlockSpec((tf, d), lambda i, j: (j, 0)),
                  pl.BlockSpec((1, d), vec)],
        out_specs=pl.BlockSpec((tm, d), row),
        out_shape=jax.ShapeDtypeStruct((n, d), F32),
        scratch_shapes=[pltpu.VMEM((tm, d), BF16), pltpu.VMEM((tm, d), F32)],
        compiler_params=_params(("parallel", "arbitrary"), 56),
        name="ffn",
    )(x, pre_g, sh, sc, gate, wg, wu, wd, post_g)


def _in_proj_kernel(x_ref, pre_g_ref, sh_ref, sc_ref, w_ref, cos_ref, sin_ref,
                    q1_ref, k1_ref, v1_ref, q4_ref, k4_ref, v4_ref, q16_ref, k16_ref, v16_ref,
                    u_ref, h_ref, cval_ref, slab_ref):
    j = pl.program_id(1)
    tm = x_ref.shape[0]

    @pl.when(j == 0)
    def _():
        h = _rms(x_ref[...], pre_g_ref[...]) * (1.0 + sc_ref[...]) + sh_ref[...]
        h_ref[...] = h.astype(BF16)

    p = jnp.dot(h_ref[...], w_ref[...], preferred_element_type=F32)
    n_heads = p.shape[1] // HEAD_DIM

    def emit(nat_ref, class_refs, rotary, scale):
        cos = cos_ref[...]
        sin = sin_ref[...]
        for hd in range(n_heads):
            sl = slice(hd * HEAD_DIM, (hd + 1) * HEAD_DIM)
            t = p[:, sl]
            if rotary:
                t = t * cos + pltpu.roll(t, HEAD_DIM // 2, axis=1) * sin
            if scale != 1.0:
                t = t * scale
            nat_ref[:, sl] = t.astype(nat_ref.dtype)
            slab_ref[hd] = t
        for (_, dil), c_ref in zip(DILATED_PATTERNS[1:], class_refs):
            for hd in range(n_heads):
                for r in range(dil):
                    rows = slab_ref[hd, pl.ds(r, tm // dil, stride=dil), :]
                    c_ref[hd, r] = rows.astype(c_ref.dtype)

    @pl.when(j == 0)
    def _():
        emit(q1_ref, (q4_ref, q16_ref), True, HEAD_DIM ** -0.5)

    @pl.when(j == 1)
    def _():
        emit(k1_ref, (k4_ref, k16_ref), True, 1.0)

    @pl.when(j == 2)
    def _():
        emit(v1_ref, (v4_ref, v16_ref), False, 1.0)

    @pl.when(j == 3)
    def _():
        cval_ref[...] = p

    @pl.when(j == 4)
    def _():
        u_ref[...] = cval_ref[...] * _sigmoid(p)


def _in_proj(x, pre_g, sh, sc, w5, cosf, sinf, batch, seq, tm=512):
    n, d = x.shape
    wdt = w5.shape[2]
    heads = wdt // HEAD_DIM
    tpb = seq // tm
    row = lambda i, j: (i, 0)
    vec = lambda i, j: (0, 0)
    per_b = lambda i, j: (i // tpb, 0, 0)
    pos = lambda i, j: (i % tpb, 0)
    out_specs = [pl.BlockSpec((tm, wdt), row)] * 3
    out_shape = [jax.ShapeDtypeStruct((n, wdt), BF16)] * 3
    for _, dil in DILATED_PATTERNS[1:]:
        spec = pl.BlockSpec((None, heads, dil, tm // dil, HEAD_DIM),
                            lambda i, j: (i // tpb, 0, 0, i % tpb, 0))
        out_specs += [spec] * 3
        out_shape += [jax.ShapeDtypeStruct((batch, heads, dil, seq // dil, HEAD_DIM), BF16)] * 3
    out_specs.append(pl.BlockSpec((tm, wdt), row))
    out_shape.append(jax.ShapeDtypeStruct((n, wdt), F32))
    return pl.pallas_call(
        _in_proj_kernel,
        grid=(n // tm, w5.shape[0]),
        in_specs=[pl.BlockSpec((tm, d), row),
                  pl.BlockSpec((1, d), vec),
                  pl.BlockSpec((None, 1, d), per_b),
                  pl.BlockSpec((None, 1, d), per_b),
                  pl.BlockSpec((None, d, wdt), lambda i, j: (j, 0, 0)),
                  pl.BlockSpec((tm, HEAD_DIM), pos),
                  pl.BlockSpec((tm, HEAD_DIM), pos)],
        out_specs=out_specs,
        out_shape=out_shape,
        scratch_shapes=[pltpu.VMEM((tm, d), BF16), pltpu.VMEM((tm, wdt), F32),
                        pltpu.VMEM((heads, tm, HEAD_DIM), F32)],
        compiler_params=_params(("parallel", "arbitrary"), 58),
        name="in_proj",
    )(x, pre_g, sh, sc, w5, cosf, sinf)


def _attn_kernel(q1_ref, k1_ref, v1_ref, q4_ref, k4_ref, v4_ref, q16_ref, k16_ref, v16_ref,
                 o_ref, acc_ref, m_ref, bias_ref):
    seq = o_ref.shape[0]
    rel = (lax.broadcasted_iota(jnp.int32, (Q_BLOCK, K_BLOCK), 1)
           - lax.broadcasted_iota(jnp.int32, (Q_BLOCK, K_BLOCK), 0))
    for variant, off in enumerate((0, -N_SIDE, -2 * N_SIDE)):
        ok = jnp.abs(rel + off) <= N_SIDE
        bias_ref[variant] = jnp.where(ok, 0.0, MASK_VALUE).astype(F32)

    ones = jnp.ones((K_BLOCK, HEAD_DIM), BF16)

    def run_pattern(refs, dil, first=False, last=False):
        q_ref, k_ref, v_ref = refs
        length = seq // dil
        nblk = length // Q_BLOCK
        shift = nblk.bit_length() - 1

        def one_block(bidx):
            r = bidx >> shift
            jb = bidx & (nblk - 1)
            m0 = pl.multiple_of(jb * Q_BLOCK, Q_BLOCK)
            ks = pl.multiple_of(jnp.clip(m0 - N_SIDE, 0, length - K_BLOCK), N_SIDE)
            variant = jnp.where(jb == 0, 0, jnp.where(jb == nblk - 1, 2, 1))
            if dil == 1:
                qb = q_ref[pl.ds(m0, Q_BLOCK), :]
                kb = k_ref[pl.ds(ks, K_BLOCK), :]
                vb = v_ref[pl.ds(ks, K_BLOCK), :]
            else:
                qb = q_ref[r, pl.ds(m0, Q_BLOCK), :]
                kb = k_ref[r, pl.ds(ks, K_BLOCK), :]
                vb = v_ref[r, pl.ds(ks, K_BLOCK), :]
            s = lax.dot_general(qb, kb, (((1,), (1,)), ((), ())),
                                preferred_element_type=F32) + bias_ref[variant]
            row_max = jnp.max(s, axis=1, keepdims=True)
            v_aug = jnp.concatenate([vb, ones], axis=1)
            if dil == 1:
                rows = pl.ds(m0, Q_BLOCK)
            else:
                rows = pl.ds(m0 * dil + r, Q_BLOCK, stride=dil)
            if first:
                m_new = jnp.broadcast_to(row_max, (Q_BLOCK, HEAD_DIM))
            else:
                m_old = m_ref[rows, :]
                m_new = jnp.maximum(m_old, row_max)
                alpha = jnp.exp(m_old - m_new)
            p = jnp.exp(s - jnp.concatenate([m_new, m_new], axis=1)).astype(BF16)
            pv = jnp.dot(p, v_aug, preferred_element_type=F32)
            num, den = pv[:, :HEAD_DIM], pv[:, HEAD_DIM:]
            if not first:
                num = acc_ref[0, rows, :] * alpha + num
                den = acc_ref[1, rows, :] * alpha + den
            if last:
                o_ref[rows, :] = (num / den).astype(o_ref.dtype)
            else:
                acc_ref[0, rows, :] = num
                acc_ref[1, rows, :] = den
                m_ref[rows, :] = m_new

        def body(it, carry):
            for g in range(BLOCKS_PER_ITER):
                one_block(it * BLOCKS_PER_ITER + g)
            return carry

        lax.fori_loop(0, (seq // Q_BLOCK) // BLOCKS_PER_ITER, body, 0)

    run_pattern((q16_ref, k16_ref, v16_ref), 16, first=True)
    run_pattern((q4_ref, k4_ref, v4_ref), 4)
    run_pattern((q1_ref, k1_ref, v1_ref), 1, last=True)


def _attention(q, k, v, classes, batch, seq, heads):
    nat = pl.BlockSpec((None, seq, HEAD_DIM), lambda b, h: (b, 0, h))
    in_specs = [nat, nat, nat]
    args = [q, k, v]
    for dil, arrs in classes:
        spec = pl.BlockSpec((None, None, dil, seq // dil, HEAD_DIM), lambda b, h: (b, h, 0, 0, 0))
        in_specs += [spec] * 3
        args += list(arrs)
    return pl.pallas_call(
        _attn_kernel,
        grid=(batch, heads),
        in_specs=in_specs,
        out_specs=nat,
        out_shape=jax.ShapeDtypeStruct((batch, seq, heads * HEAD_DIM), F32),
        scratch_shapes=[pltpu.VMEM((2, seq, HEAD_DIM), F32),
                        pltpu.VMEM((seq, HEAD_DIM), F32),
                        pltpu.VMEM((3, Q_BLOCK, K_BLOCK), F32)],
        compiler_params=_params(("parallel", "parallel"), 48),
        name="attention",
    )(*args)


def _mix_out_kernel(attn_ref, u_ref, uprev_ref, unext_ref, x_ref, cw_ref, cb_ref, lng_ref,
                    lnb_ref, ag_ref, cg_ref, wo_ref, post_g_ref, gate_ref, o_ref, ubuf_ref,
                    conv_ref, *, tiles_per_seq):
    tm, cw = u_ref.shape
    n_slabs = cw // HEAD_DIM
    t = pl.program_id(0) % tiles_per_seq
    keep_prev = (t != 0).astype(F32)
    keep_next = (t != tiles_per_seq - 1).astype(F32)
    for c in range(n_slabs):
        sl = slice(c * HEAD_DIM, (c + 1) * HEAD_DIM)
        ubuf_ref[c, pl.ds(0, CONV_HALO, stride=2), :] = uprev_ref[:, sl] * keep_prev
        ubuf_ref[c, pl.ds(2 * CONV_HALO, tm, stride=2), :] = u_ref[:, sl]
        ubuf_ref[c, pl.ds(2 * (CONV_HALO + tm), CONV_HALO, stride=2), :] = unext_ref[:, sl] * keep_next

    pad = (CONV_KERNEL - 1) // 2

    def conv_rows(it, carry):
        c = it % n_slabs
        r0 = pl.multiple_of((it // n_slabs) * CONV_ROWS, CONV_ROWS)
        groups = CONV_ROWS // 8
        taps = [jnp.broadcast_to(cw_ref[c, tap:tap + 1, :], (8, HEAD_DIM))
                for tap in range(CONV_KERNEL)]
        accs = [jnp.broadcast_to(cb_ref[c], (8, HEAD_DIM))] * groups
        for s in range(CONV_ROWS - 8 + CONV_KERNEL):
            win = ubuf_ref[c, pl.ds(2 * (r0 + CONV_HALO - pad + s), 8, stride=2), :]
            for g in range(groups):
                tap = s - 8 * g
                if 0 <= tap < CONV_KERNEL:
                    accs[g] = accs[g] + taps[tap] * win
        for g in range(groups):
            conv_ref[c, pl.ds(r0 + 8 * g, 8), :] = accs[g]
        return carry

    lax.fori_loop(0, (tm // CONV_ROWS) * n_slabs, conv_rows, 0)
    conv = jnp.concatenate([conv_ref[c] for c in range(n_slabs)], axis=1)

    mu = jnp.mean(conv, axis=-1, keepdims=True)
    xc = conv - mu
    y = xc * lax.rsqrt(jnp.mean(xc * xc, axis=-1, keepdims=True) + NORM_EPS)
    y = y * lng_ref[...] + lnb_ref[...]
    y = y * _sigmoid(y)
    a_conv = _rms(y, cg_ref[...]).astype(BF16)
    a_attn = _rms(attn_ref[...], ag_ref[...]).astype(BF16)
    aw = attn_ref.shape[1]
    merged = (jnp.dot(a_attn, wo_ref[0:aw, :], preferred_element_type=F32)
              + jnp.dot(a_conv, wo_ref[aw:, :], preferred_element_type=F32))
    o_ref[...] = x_ref[...] + gate_ref[...] * _rms(merged, post_g_ref[...])


def _mix_out(attn, u, x, conv_w, conv_b, ln_g, ln_b, attn_g, conv_g, w_out, post_g, gate,
             seq, tm=512):
    n, d = x.shape
    aw = attn.shape[1]
    cw = u.shape[1]
    tpb = seq // tm
    hb = tm // CONV_HALO
    last_halo = n // CONV_HALO - 1
    row = lambda i: (i, 0)
    vec = lambda i: (0, 0)
    return pl.pallas_call(
        functools.partial(_mix_out_kernel, tiles_per_seq=tpb),
        grid=(n // tm,),
        in_specs=[pl.BlockSpec((tm, aw), row),
                  pl.BlockSpec((tm, cw), row),
                  pl.BlockSpec((CONV_HALO, cw), lambda i: (jnp.maximum(i * hb - 1, 0), 0)),
                  pl.BlockSpec((CONV_HALO, cw), lambda i: (jnp.minimum((i + 1) * hb, last_halo), 0)),
                  pl.BlockSpec((tm, d), row),
                  pl.BlockSpec((cw // HEAD_DIM, CONV_KERNEL, HEAD_DIM), lambda i: (0, 0, 0)),
                  pl.BlockSpec((cw // HEAD_DIM, 1, HEAD_DIM), lambda i: (0, 0, 0)),
                  pl.BlockSpec((1, cw), vec),
                  pl.BlockSpec((1, cw), vec),
                  pl.BlockSpec((1, aw), vec),
                  pl.BlockSpec((1, cw), vec),
                  pl.BlockSpec((aw + cw, d), vec),
                  pl.BlockSpec((1, d), vec),
                  pl.BlockSpec((None, 1, d), lambda i: (i // tpb, 0, 0))],
        out_specs=pl.BlockSpec((tm, d), row),
        out_shape=jax.ShapeDtypeStruct((n, d), F32),
        scratch_shapes=[pltpu.VMEM((cw // HEAD_DIM, 2 * (tm + 2 * CONV_HALO), HEAD_DIM), F32),
                        pltpu.VMEM((cw // HEAD_DIM, tm, HEAD_DIM), F32)],
        compiler_params=_params(("parallel",), 56),
        name="mix_out",
    )(attn, u, u, u, x, conv_w, conv_b, ln_g, ln_b, attn_g, conv_g, w_out, post_g, gate)


def _rope_tables(seq):
    half = HEAD_DIM // 2
    inv_freq = ROPE_THETA ** (-jnp.arange(half, dtype=F32) / half)
    ang = jnp.arange(seq, dtype=jnp.int32).astype(F32)[:, None] * inv_freq[None, :]
    cos, sin = jnp.cos(ang), jnp.sin(ang)
    return jnp.concatenate([cos, cos], axis=-1), jnp.concatenate([-sin, sin], axis=-1)


def kernel(x, c, w_ada, b_ada, ffn1_pre_g, ffn1_w_gate, ffn1_w_up, ffn1_w_down, ffn1_post_g, mix_pre_g, w_in, conv_w, conv_b, conv_ln_g, conv_ln_b, attn_out_g, conv_out_g, w_out, mix_post_g, ffn2_pre_g, ffn2_w_gate, ffn2_w_up, ffn2_w_down, ffn2_post_g):
    batch, seq, d = x.shape
    depth = w_ada.shape[0]
    attn_w = attn_out_g.shape[1]
    heads = attn_w // HEAD_DIM
    n = batch * seq
    cosf, sinf = _rope_tables(seq)
    c8 = jnp.pad(c, ((0, 8 - batch), (0, 0)))
    row = lambda a: a.reshape(1, -1)
    xf = x.reshape(n, d)

    for l in range(depth):
        mod = _adaln(c8, w_ada[l], row(b_ada[l]))[:batch]
        sh1, sc1, g1, sh2, sc2, g2, sh3, sc3, g3 = [
            m.reshape(batch, 1, d) for m in jnp.split(mod, N_MOD, axis=-1)]

        xf = _ffn(xf, row(ffn1_pre_g[l]), sh1, sc1, g1, ffn1_w_gate[l].astype(BF16),
                  ffn1_w_up[l].astype(BF16), ffn1_w_down[l].astype(BF16),
                  row(ffn1_post_g[l]), seq)

        w5 = w_in[l].reshape(d, -1, attn_w).transpose(1, 0, 2).astype(BF16)
        proj = _in_proj(xf, row(mix_pre_g[l]), sh2, sc2, w5, cosf, sinf, batch, seq)
        u = proj[-1]
        shp = (batch, seq, attn_w)
        classes = [(dil, proj[3 * (n_p + 1):3 * (n_p + 2)])
                   for n_p, (_, dil) in enumerate(DILATED_PATTERNS[1:])]
        attn = _attention(*[t.reshape(shp) for t in proj[:3]], classes, batch, seq, heads)

        conv_w_slabs = conv_w[l].reshape(CONV_KERNEL, -1, HEAD_DIM).transpose(1, 0, 2)
        xf = _mix_out(attn.reshape(n, attn_w), u, xf, conv_w_slabs,
                      conv_b[l].reshape(-1, 1, HEAD_DIM),
                      row(conv_ln_g[l]), row(conv_ln_b[l]), row(attn_out_g[l]),
                      row(conv_out_g[l]), w_out[l].astype(BF16), row(mix_post_g[l]), g2, seq)

        xf = _ffn(xf, row(ffn2_pre_g[l]), sh3, sc3, g3, ffn2_w_gate[l].astype(BF16),
                  ffn2_w_up[l].astype(BF16), ffn2_w_down[l].astype(BF16),
                  row(ffn2_post_g[l]), seq)

    return xf.reshape(batch, seq, d)
```
